```python
import math
import jax, jax.numpy as jnp
from jax import lax
import numpy as np

D_MODEL = 4096
BATCH = 16
SEQ = 256
DEPTH = 2
DEC_BATCH = 8
DEC_SEQ = 4096
PAST_LEN = 256

GRID_W = 64
HEAD_DIM = 128
MIX_W = D_MODEL
N_MOD = 9
D_FF = 11008
EPS = 1e-6
Q_BLOCK = 128
W_A = MIX_W // 2
H_A = W_A // HEAD_DIM
WIN_R = 8
WIN_C = 16
COL_BLK = 16
COL_BAND = 32
W_B = MIX_W // 2
LRU_BLOCK = 256
NB_B = W_B // LRU_BLOCK
CONV_W = 4
LRU_C = 8.0
H_C = MIX_W // HEAD_DIM
KV_C = H_C // 4
ROPE_THETA = 10000.0
N_EVEN = (DEPTH + 1) // 2
N_ODD = DEPTH // 2

F32 = jnp.float32

kernel_name = 'hybrid_na_rglru_gqa_diffusion_step'


def rmsnorm(x, g):
    xf = x.astype(F32)
    y = xf * lax.rsqrt(jnp.mean(xf * xf, axis=-1, keepdims=True) + EPS)
    return (y * g.astype(F32)).astype(x.dtype)


def modulation(cond, w, b):
    m = jax.nn.silu(cond) @ w + b
    return m.reshape(cond.shape[0], N_MOD, D_MODEL)


def adaln(x, g, mod, j):
    return rmsnorm(x, g) * (1 + mod[:, 3 * j + 1, None]) + mod[:, 3 * j, None]


def gate(mod, j):
    return mod[:, 3 * j + 2, None]


def half_ffn(x, g, mod, j, w1, w3, w2):
    h = adaln(x, g, mod, j)
    f = (jax.nn.silu(h @ w1) * (h @ w3)) @ w2
    return x + 0.5 * gate(mod, j) * f


def attend_blocks(q, k, v):
    B, Tq, H, hd = q.shape
    Hk = k.shape[2]
    G = H // Hk
    nb = Tq // Q_BLOCK
    scale = hd ** -0.5
    qb = q.reshape(B, nb, Q_BLOCK, Hk, G, hd).transpose(1, 0, 2, 3, 4, 5)

    def one_block(q_i):
        s = jnp.einsum('bqkgd,bskd->bkgqs', q_i, k, preferred_element_type=F32) * scale
        p = jax.nn.softmax(s, axis=-1).astype(v.dtype)
        return jnp.einsum('bkgqs,bskd->bqkgd', p, v)

    o = lax.map(one_block, qb)
    return o.transpose(1, 0, 2, 3, 4, 5).reshape(B, Tq, H, hd)


def rope_2d(x):
    T = x.shape[1]
    t = jnp.arange(T)
    half = HEAD_DIM // 2
    inv = ROPE_THETA ** (-jnp.arange(0, half, 2, dtype=F32) / half)
    xf = x.astype(F32)

    def rot(xa, pos):
        ang = pos.astype(F32)[:, None] * inv
        cos = jnp.cos(ang)[:, None]
        sin = jnp.sin(ang)[:, None]
        x1, x2 = jnp.split(xa, 2, axis=-1)
        return jnp.concatenate([x1 * cos - x2 * sin, x1 * sin + x2 * cos], axis=-1)

    out = jnp.concatenate([rot(xf[..., :half], t // GRID_W), rot(xf[..., half:], t % GRID_W)], axis=-1)
    return out.astype(x.dtype)


def neighbourhood_attention(q, k, v, ck, cv, rpb):
    B, T, H, hd = q.shape
    rows = T // GRID_W
    kr = min(WIN_R, rows)
    n_cb = GRID_W // COL_BLK
    scale = hd ** -0.5
    qc = np.arange(GRID_W).reshape(n_cb, COL_BLK)
    cs = np.clip(qc - WIN_C // 2, 0, GRID_W - WIN_C)
    band = (np.clip(np.arange(n_cb) * COL_BLK - WIN_C // 2, 0, GRID_W - COL_BAND)[:, None]
            + np.arange(COL_BAND)[None, :])
    kcol = band[:, None, :]
    col_mask = (kcol >= cs[:, :, None]) & (kcol < cs[:, :, None] + WIN_C)
    col_idx = np.clip(kcol - qc[:, :, None] + WIN_C - 1, 0, 2 * WIN_C - 2)
    row_start = jnp.asarray(np.clip(np.arange(rows) - kr // 2, 0, rows - kr), dtype=jnp.int32)
    mask = jnp.asarray(col_mask)[None, None, :, :, None, :]
    kg = k.reshape(B, rows, GRID_W, H, hd)
    vg = v.reshape(B, rows, GRID_W, H, hd)
    qg = q.reshape(B, rows, n_cb, COL_BLK, H, hd).transpose(1, 0, 2, 3, 4, 5)
    n_loc = kr * COL_BAND

    def one_row(args):
        r, q_r = args
        rs = row_start[r]
        k_band = lax.dynamic_slice_in_dim(kg, rs, kr, axis=1)[:, :, band]
        v_band = lax.dynamic_slice_in_dim(vg, rs, kr, axis=1)[:, :, band]
        row_idx = rs + jnp.arange(kr) - r + WIN_R - 1
        bias = rpb[:, row_idx][:, :, col_idx].astype(F32)
        bias = bias.transpose(0, 2, 3, 1, 4)[None]
        s_loc = jnp.einsum('bnqhd,brnkhd->bhnqrk', q_r, k_band, preferred_element_type=F32) * scale + bias
        s_loc = jnp.where(mask, s_loc, -jnp.inf).reshape(B, H, n_cb, COL_BLK, n_loc)
        s_ctx = jnp.einsum('bnqhd,bphd->bhnqp', q_r, ck, preferred_element_type=F32) * scale
        p = jax.nn.softmax(jnp.concatenate([s_loc, s_ctx], axis=-1), axis=-1).astype(v.dtype)
        p_loc = p[..., :n_loc].reshape(B, H, n_cb, COL_BLK, kr, COL_BAND)
        return (jnp.einsum('bhnqrk,brnkhd->bnqhd', p_loc, v_band)
                + jnp.einsum('bhnqp,bphd->bnqhd', p[..., n_loc:], cv))

    out = lax.map(one_row, (jnp.arange(rows, dtype=jnp.int32), qg))
    return out.transpose(1, 0, 2, 3, 4, 5).reshape(B, T, H, hd)


def centred_dwconv(x, w, b):
    T = x.shape[1]
    left = CONV_W // 2
    xp = jnp.pad(x, ((0, 0), (left, CONV_W - 1 - left), (0, 0)))
    y = xp[:, 0:T] * w[0]
    for j in range(1, CONV_W):
        y = y + xp[:, j:j + T] * w[j]
    return y + b


def linear_scan(a, u, h0, reverse):
    def step(h, au):
        a_t, u_t = au
        h = a_t * h + u_t
        return h, h
    h_last, hs = lax.scan(step, h0, (a.transpose(1, 0, 2), u.transpose(1, 0, 2)), reverse=reverse)
    return hs.transpose(1, 0, 2), h_last


def rglru_bidir(xb, conv_w, conv_b, wa, ba, wx, bx, lam, h0):
    B, T, _ = xb.shape
    xc = centred_dwconv(xb, conv_w, conv_b)
    xcb = xc.reshape(B, T, NB_B, LRU_BLOCK)
    ys = []
    finals = []
    for d in range(2):
        r = jax.nn.sigmoid(jnp.einsum('btnc,ncd->btnd', xcb, wa[d]).reshape(B, T, W_B) + ba[d]).astype(F32)
        i = jax.nn.sigmoid(jnp.einsum('btnc,ncd->btnd', xcb, wx[d]).reshape(B, T, W_B) + bx[d])
        log_a = -LRU_C * r * jax.nn.softplus(-lam[d].astype(F32))
        a = jnp.exp(log_a)
        u = jnp.sqrt(-jnp.expm1(2 * log_a)) * (i * xc).astype(F32)
        hs, h_last = linear_scan(a, u, h0[:, d], d == 1)
        ys.append(hs)
        finals.append(h_last)
    y = (ys[0] + ys[1]).astype(xb.dtype)
    return y, jnp.stack(finals, axis=1)


def even_project(h, w_in):
    B, T, _ = h.shape
    p = h @ w_in
    q, k, v, xb, gb = jnp.split(p, [W_A, 2 * W_A, 3 * W_A, 3 * W_A + W_B], axis=-1)
    shp = (B, T, H_A, HEAD_DIM)
    return q.reshape(shp), k.reshape(shp), v.reshape(shp), xb, gb


def even_output(oa, yb, gb, w_out):
    B, T = yb.shape[0], yb.shape[1]
    merged = jnp.concatenate([oa.reshape(B, T, W_A), yb * jax.nn.gelu(gb)], axis=-1)
    return merged @ w_out


def odd_project(h, w_in, q_g, k_g):
    B, T, _ = h.shape
    p = h @ w_in
    q, k, v = jnp.split(p, [H_C * HEAD_DIM, (H_C + KV_C) * HEAD_DIM], axis=-1)
    q = rmsnorm(q.reshape(B, T, H_C, HEAD_DIM), q_g)
    k = rmsnorm(k.reshape(B, T, KV_C, HEAD_DIM), k_g)
    return q, k, v.reshape(B, T, KV_C, HEAD_DIM)


def setup_inputs(seed: int = 0) -> dict:
    key = jax.random.key(seed)
    ks = jax.random.split(key, 32)
    D = D_MODEL

    def nrm(k, shape, s):
        return jax.random.normal(k, shape, F32) * s

    u = jax.random.uniform(ks[20], (N_EVEN, 2, W_B), F32, 0.9, 0.999)
    sg = u ** (1.0 / LRU_C)
    return {
        'x_prompt': nrm(ks[0], (BATCH, SEQ, D), 1.0),
        'x_sample': nrm(ks[1], (DEC_BATCH, DEC_SEQ, D), 1.0),
        'cache_a_k': nrm(ks[2], (DEC_BATCH, N_EVEN, PAST_LEN, H_A, HEAD_DIM), 1.0),
        'cache_a_v': nrm(ks[3], (DEC_BATCH, N_EVEN, PAST_LEN, H_A, HEAD_DIM), 1.0),
        'state_lru': nrm(ks[4], (DEC_BATCH, N_EVEN, 2, W_B), 0.5),
        'cache_c_k': nrm(ks[5], (DEC_BATCH, N_ODD, PAST_LEN, KV_C, HEAD_DIM), 1.0),
        'cache_c_v': nrm(ks[6], (DEC_BATCH, N_ODD, PAST_LEN, KV_C, HEAD_DIM), 1.0),
        'c': nrm(ks[7], (DEC_BATCH, D), 1.0),
        'c_ctx': nrm(ks[8], (D,), 1.0),
        'ada_w': nrm(ks[9], (DEPTH, D, N_MOD * D), D ** -0.5),
        'ada_b': nrm(ks[10], (DEPTH, N_MOD * D), 0.01),
        'norm_g': 1.0 + nrm(ks[11], (DEPTH, 3, D), 0.01),
        'ffn_w1': nrm(ks[12], (DEPTH, 2, D, D_FF), D ** -0.5),
        'ffn_w3': nrm(ks[13], (DEPTH, 2, D, D_FF), D ** -0.5),
        'ffn_w2': nrm(ks[14], (DEPTH, 2, D_FF, D), D_FF ** -0.5),
        'ev_w_in': nrm(ks[15], (N_EVEN, D, 3 * W_A + 2 * W_B), D ** -0.5),
        'ev_w_out': nrm(ks[16], (N_EVEN, W_A + W_B, D), (W_A + W_B) ** -0.5),
        'na_rpb': nrm(ks[17], (N_EVEN, H_A, 2 * WIN_R - 1, 2 * WIN_C - 1), 0.02),
        'lru_conv_w': nrm(ks[18], (N_EVEN, CONV_W, W_B), CONV_W ** -0.5),
        'lru_conv_b': nrm(ks[19], (N_EVEN, W_B), 0.01),
        'lru_wa': nrm(ks[21], (N_EVEN, 2, NB_B, LRU_BLOCK, LRU_BLOCK), LRU_BLOCK ** -0.5),
        'lru_ba': nrm(ks[22], (N_EVEN, 2, W_B), 0.01),
        'lru_wx': nrm(ks[23], (N_EVEN, 2, NB_B, LRU_BLOCK, LRU_BLOCK), LRU_BLOCK ** -0.5),
        'lru_bx': nrm(ks[24], (N_EVEN, 2, W_B), 0.01),
        'lru_lambda': jnp.log(sg) - jnp.log1p(-sg),
        'od_w_in': nrm(ks[25], (N_ODD, D, (H_C + 2 * KV_C) * HEAD_DIM), D ** -0.5),
        'od_w_out': nrm(ks[26], (N_ODD, H_C * HEAD_DIM, D), (H_C * HEAD_DIM) ** -0.5),
        'c_q_g': 1.0 + nrm(ks[27], (N_ODD, HEAD_DIM), 0.01),
        'c_k_g': 1.0 + nrm(ks[28], (N_ODD, HEAD_DIM), 0.01),
        'norm_f': 1.0 + nrm(ks[29], (D,), 0.01),
    }


def reference(x_prompt, x_sample, cache_a_k, cache_a_v, state_lru, cache_c_k, cache_c_v, c, c_ctx,
              ada_w, ada_b, norm_g, ffn_w1, ffn_w3, ffn_w2, ev_w_in, ev_w_out, na_rpb,
              lru_conv_w, lru_conv_b, lru_wa, lru_ba, lru_wx, lru_bx, lru_lambda,
              od_w_in, od_w_out, c_q_g, c_k_g, norm_f):
    xp = x_prompt
    xs = x_sample
    new_a_k, new_a_v, new_st, new_c_k, new_c_v = [], [], [], [], []
    for l in range(DEPTH):
        mod_p = modulation(c_ctx[None], ada_w[l], ada_b[l])
        mod_s = modulation(c, ada_w[l], ada_b[l])
        xp = half_ffn(xp, norm_g[l, 0], mod_p, 0, ffn_w1[l, 0], ffn_w3[l, 0], ffn_w2[l, 0])
        xs = half_ffn(xs, norm_g[l, 0], mod_s, 0, ffn_w1[l, 0], ffn_w3[l, 0], ffn_w2[l, 0])
        hp = adaln(xp, norm_g[l, 1], mod_p, 1)
        hs = adaln(xs, norm_g[l, 1], mod_s, 1)
        if l % 2 == 0:
            e = l // 2
            lru = (lru_conv_w[e], lru_conv_b[e], lru_wa[e], lru_ba[e], lru_wx[e], lru_bx[e], lru_lambda[e])
            q, k, v, xb, gb = even_project(hp, ev_w_in[e])
            oa = attend_blocks(q, k, v)
            yb, st = rglru_bidir(xb, *lru, jnp.zeros((xb.shape[0], 2, W_B), F32))
            mp = even_output(oa, yb, gb, ev_w_out[e])
            new_a_k.append(k)
            new_a_v.append(v)
            new_st.append(st.astype(x_prompt.dtype))
            q, k, v, xb, gb = even_project(hs, ev_w_in[e])
            oa = neighbourhood_attention(q, k, v, cache_a_k[:, e], cache_a_v[:, e], na_rpb[e])
            yb, _ = rglru_bidir(xb, *lru, state_lru[:, e].astype(F32))
            ms = even_output(oa, yb, gb, ev_w_out[e])
        else:
            o = l // 2
            q, k, v = odd_project(hp, od_w_in[o], c_q_g[o], c_k_g[o])
            mp = attend_blocks(q, k, v).reshape(xp.shape[0], xp.shape[1], H_C * HEAD_DIM) @ od_w_out[o]
            new_c_k.append(k)
            new_c_v.append(v)
            q, k, v = odd_project(hs, od_w_in[o], c_q_g[o], c_k_g[o])
            k_all = jnp.concatenate([rope_2d(k), cache_c_k[:, o].astype(k.dtype)], axis=1)
            v_all = jnp.concatenate([v, cache_c_v[:, o].astype(v.dtype)], axis=1)
            ms = attend_blocks(rope_2d(q), k_all, v_all).reshape(xs.shape[0], xs.shape[1], H_C * HEAD_DIM) @ od_w_out[o]
        xp = xp + gate(mod_p, 1) * mp
        xs = xs + gate(mod_s, 1) * ms
        xp = half_ffn(xp, norm_g[l, 2], mod_p, 2, ffn_w1[l, 1], ffn_w3[l, 1], ffn_w2[l, 1])
        xs = half_ffn(xs, norm_g[l, 2], mod_s, 2, ffn_w1[l, 1], ffn_w3[l, 1], ffn_w2[l, 1])
    y_prompt = rmsnorm(xp, norm_f)
    y_sample = rmsnorm(xs, norm_f)
    new_cache_a_k = jnp.stack(new_a_k, axis=1)
    new_cache_a_v = jnp.stack(new_a_v, axis=1)
    new_state_lru = jnp.stack(new_st, axis=1)
    new_cache_c_k = jnp.stack(new_c_k, axis=1)
    new_cache_c_v = jnp.stack(new_c_v, axis=1)
    return (y_prompt, y_sample, new_cache_a_k, new_cache_a_v, new_state_lru, new_cache_c_k, new_cache_c_v)
```

```python
import functools
import math

import numpy as np
import jax
import jax.numpy as jnp
from jax import lax
from jax.experimental import pallas as pl
from jax.experimental.pallas import tpu as pltpu

F32 = jnp.float32
BF16 = jnp.bfloat16

GRID_W = 64
N_MOD = 9
EPS = 1e-6
ROPE_THETA = 10000.0
LRU_C = 8.0
NEG_MASK = -1e30

V7X_VMEM_BYTES = 64 * 2**20
V7X_SCOPED_VMEM_MAX = 60000 * 1024
LANES = 128
SUBLANES = 8
MXU_DIM = 256

COND_ROWS = 16


def _pick(n, pref, align):
    if n <= pref:
        return n
    t = (pref // align) * align
    while t >= align:
        if n % t == 0:
            return t
        t -= align
    raise ValueError(f"no block of {n} aligned to {align} below {pref}")


def _params(semantics, *buffer_bytes):
    need = int(sum(buffer_bytes))
    limit = min(max(need, 16 * 2**20), V7X_SCOPED_VMEM_MAX)
    return pltpu.CompilerParams(dimension_semantics=semantics, vmem_limit_bytes=limit)


def _nbytes(shape, dtype):
    return int(np.prod(shape)) * jnp.dtype(dtype).itemsize


def _mod_kernel(c_ref, w_ref, b_ref, o_ref):
    c = c_ref[...]
    s = (c * jax.nn.sigmoid(c)).astype(BF16)
    w = w_ref[...].astype(BF16)
    o_ref[...] = jnp.dot(s, w, preferred_element_type=F32) + b_ref[...]


def _modulation(cond, ada_w, ada_b):
    depth, d, n = ada_w.shape
    r = cond.shape[0]
    tn = _pick(n, 512, LANES)
    return pl.pallas_call(
        _mod_kernel,
        grid=(depth, n // tn),
        in_specs=[
            pl.BlockSpec((r, d), lambda l, j: (0, 0)),
            pl.BlockSpec((None, d, tn), lambda l, j: (l, 0, j)),
            pl.BlockSpec((None, 1, tn), lambda l, j: (l, 0, j)),
        ],
        out_specs=pl.BlockSpec((None, r, tn), lambda l, j: (l, 0, j)),
        out_shape=jax.ShapeDtypeStruct((depth, r, n), F32),
        compiler_params=_params(("parallel", "parallel"),
                                2 * _nbytes((d, tn), F32), _nbytes((d, tn), BF16),
                                4 * _nbytes((r, tn), F32), 2 * _nbytes((r, d), F32)),
        name="modulation",
    )(cond, ada_w, ada_b.reshape(depth, 1, n))


def _adaln_kernel(x_ref, g_ref, sh_ref, sc_ref, o_ref):
    x = x_ref[...]
    y = x * lax.rsqrt(jnp.mean(x * x, axis=-1, keepdims=True) + EPS) * g_ref[...]
    o_ref[...] = (y * (1.0 + sc_ref[...]) + sh_ref[...]).astype(o_ref.dtype)


def _rms_kernel(x_ref, g_ref, o_ref):
    x = x_ref[...]
    y = x * lax.rsqrt(jnp.mean(x * x, axis=-1, keepdims=True) + EPS) * g_ref[...]
    o_ref[...] = y.astype(o_ref.dtype)


class _Stream:
    def __init__(self, cond_base, rows_per_cond):
        self.cond_base = cond_base
        self.rows_per_cond = rows_per_cond


def _mod_row(stream, layer, tile_rows):
    tiles_per_cond = stream.rows_per_cond // tile_rows
    base = layer * COND_ROWS + stream.cond_base
    return lambda i: (base + i // tiles_per_cond) * N_MOD


def _adaln(x, g, mod3, stream, layer, slot):
    m, d = x.shape
    tt = _pick(stream.rows_per_cond, 256, SUBLANES)
    row = _mod_row(stream, layer, tt)
    return pl.pallas_call(
        _adaln_kernel,
        grid=(m // tt,),
        in_specs=[
            pl.BlockSpec((tt, d), lambda i: (i, 0)),
            pl.BlockSpec((1, d), lambda i: (0, 0)),
            pl.BlockSpec((None, 1, d), lambda i: (row(i) + 3 * slot, 0, 0)),
            pl.BlockSpec((None, 1, d), lambda i: (row(i) + 3 * slot + 1, 0, 0)),
        ],
        out_specs=pl.BlockSpec((tt, d), lambda i: (i, 0)),
        out_shape=jax.ShapeDtypeStruct((m, d), BF16),
        compiler_params=_params(("parallel",), 4 * _nbytes((tt, d), F32), 2 * _nbytes((tt, d), BF16)),
        name="adaln",
    )(x, g.reshape(1, d), mod3, mod3)


def _final_norm(x, g):
    m, d = x.shape
    tt = _pick(m, 256, SUBLANES)
    return pl.pallas_call(
        _rms_kernel,
        grid=(m // tt,),
        in_specs=[pl.BlockSpec((tt, d), lambda i: (i, 0)), pl.BlockSpec((1, d), lambda i: (0, 0))],
        out_specs=pl.BlockSpec((tt, d), lambda i: (i, 0)),
        out_shape=jax.ShapeDtypeStruct((m, d), F32),
        compiler_params=_params(("parallel",), 6 * _nbytes((tt, d), F32)),
        name="final_norm",
    )(x, g.reshape(1, d))


def _gateup_kernel(h_ref, w1_ref, w3_ref, o_ref):
    h = h_ref[...]
    a = jnp.dot(h, w1_ref[...], preferred_element_type=F32)
    b = jnp.dot(h, w3_ref[...], preferred_element_type=F32)
    o_ref[...] = ((a * jax.nn.sigmoid(a)) * b).astype(o_ref.dtype)


def _gateup(h, w1, w3, li):
    m, d = h.shape
    f = w1.shape[-1]
    tm = _pick(m, 1024, SUBLANES)
    tn = _pick(f, 512, LANES)
    return pl.pallas_call(
        _gateup_kernel,
        grid=(m // tm, f // tn),
        in_specs=[
            pl.BlockSpec((tm, d), lambda i, j: (i, 0)),
            pl.BlockSpec((None, d, tn), lambda i, j: (li, 0, j)),
            pl.BlockSpec((None, d, tn), lambda i, j: (li, 0, j)),
        ],
        out_specs=pl.BlockSpec((tm, tn), lambda i, j: (i, j)),
        out_shape=jax.ShapeDtypeStruct((m, f), BF16),
        compiler_params=_params(("parallel", "arbitrary"),
                                2 * _nbytes((tm, d), BF16), 4 * _nbytes((d, tn), BF16),
                                2 * _nbytes((tm, tn), BF16), 6 * _nbytes((tm, tn), F32)),
        name="ffn_gate_up",
    )(h, w1, w3)


def _proj_kernel(a_ref, w_ref, o_ref):
    o_ref[...] = jnp.dot(a_ref[...], w_ref[...], preferred_element_type=F32).astype(o_ref.dtype)


def _proj(a, w):
    m, k = a.shape
    n = w.shape[-1]
    tm = _pick(m, 1024, SUBLANES)
    tn = _pick(n, 1024, LANES)
    return pl.pallas_call(
        _proj_kernel,
        grid=(m // tm, n // tn),
        in_specs=[pl.BlockSpec((tm, k), lambda i, j: (i, 0)), pl.BlockSpec((k, tn), lambda i, j: (0, j))],
        out_specs=pl.BlockSpec((tm, tn), lambda i, j: (i, j)),
        out_shape=jax.ShapeDtypeStruct((m, n), F32),
        compiler_params=_params(("parallel", "arbitrary"),
                                2 * _nbytes((tm, k), BF16), 2 * _nbytes((k, tn), BF16),
                                4 * _nbytes((tm, tn), F32)),
        name="mixer_in_proj",
    )(a, w)


def _resid_kernel(a_ref, w_ref, x_ref, gate_ref, o_ref, acc_ref, *, nk, coef):
    k = pl.program_id(2)
    part = jnp.dot(a_ref[...], w_ref[...], preferred_element_type=F32)

    def finish(total):
        o_ref[...] = x_ref[...] + (coef * gate_ref[...]) * total

    if nk == 1:
        finish(part)
    else:
        @pl.when(k == 0)
        def _():
            acc_ref[...] = part

        @pl.when(jnp.logical_and(k > 0, k < nk - 1))
        def _():
            acc_ref[...] += part

        @pl.when(k == nk - 1)
        def _():
            finish(acc_ref[...] + part)


def _matmul_residual(a, w, li, x, mod3, stream, layer, slot, coef, tk_pref, tn_pref):
    m, kdim = a.shape
    d = w.shape[-1]
    tm = _pick(stream.rows_per_cond, 1024, SUBLANES)
    tn = _pick(d, tn_pref, LANES)
    tk = _pick(kdim, tk_pref, MXU_DIM)
    nk = kdim // tk
    row = _mod_row(stream, layer, tm)
    return pl.pallas_call(
        functools.partial(_resid_kernel, nk=nk, coef=coef),
        grid=(m // tm, d // tn, nk),
        in_specs=[
            pl.BlockSpec((tm, tk), lambda i, j, k: (i, k)),
            pl.BlockSpec((None, tk, tn), lambda i, j, k: (li, k, j)),
            pl.BlockSpec((tm, tn), lambda i, j, k: (i, j)),
            pl.BlockSpec((None, 1, tn), lambda i, j, k: (row(i) + 3 * slot + 2, 0, j)),
        ],
        out_specs=pl.BlockSpec((tm, tn), lambda i, j, k: (i, j)),
        out_shape=jax.ShapeDtypeStruct((m, d), F32),
        scratch_shapes=[pltpu.VMEM((tm, tn), F32)],
        compiler_params=_params(("parallel", "parallel", "arbitrary"),
                                2 * _nbytes((tm, tk), BF16), 2 * _nbytes((tk, tn), BF16),
                                7 * _nbytes((tm, tn), F32)),
        name="matmul_residual",
    )(a, w, x, mod3)


def _ctx_attn_kernel(q_ref, k_ref, v_ref, o_ref, *, scale):
    q = q_ref[...].astype(BF16)
    k = k_ref[...].astype(BF16)
    v = v_ref[...].astype(BF16)
    s = lax.dot_general(q, k, (((1,), (1,)), ((), ())), preferred_element_type=F32) * scale
    m = jnp.max(s, axis=-1, keepdims=True)
    p = jnp.exp(s - m)
    l = jnp.sum(p, axis=-1, keepdims=True)
    o = jnp.dot(p.astype(BF16), v, preferred_element_type=F32) / l
    o_ref[...] = o.astype(o_ref.dtype)


def _ctx_attention(q_arr, q_col0, k_arr, k_col0, v_arr, v_col0, batch, seq, n_heads, group, hd):
    return pl.pallas_call(
        functools.partial(_ctx_attn_kernel, scale=hd ** -0.5),
        grid=(batch, n_heads),
        in_specs=[
            pl.BlockSpec((seq, hd), lambda b, h: (b, q_col0 + h)),
            pl.BlockSpec((seq, hd), lambda b, h: (b, k_col0 + h // group)),
            pl.BlockSpec((seq, hd), lambda b, h: (b, v_col0 + h // group)),
        ],
        out_specs=pl.BlockSpec((seq, hd), lambda b, h: (b, h)),
        out_shape=jax.ShapeDtypeStruct((batch * seq, n_heads * hd), BF16),
        compiler_params=_params(("parallel", "parallel"), 8 * _nbytes((seq, hd), F32),
                                4 * _nbytes((seq, seq), F32)),
        name="context_attention",
    )(q_arr, k_arr, v_arr)


NA_ROW_BLOCK = 4


def _na_window(rows, win_r):
    kr = min(win_r, rows)
    n_blk = rows // NA_ROW_BLOCK
    r = np.arange(rows)
    row_start = np.clip(r - kr // 2, 0, rows - kr)
    lo = row_start.reshape(n_blk, NA_ROW_BLOCK).min(axis=1)
    hi = (row_start + kr).reshape(n_blk, NA_ROW_BLOCK).max(axis=1)
    n_win = int((hi - lo).max())
    n_win = min(rows, -(-n_win // 2) * 2)
    win_start = np.clip(lo, 0, rows - n_win)
    assert np.all(win_start <= lo) and np.all(win_start + n_win >= hi)
    return kr, row_start, win_start, n_win


def _na_bias_tables(rpb, rows, win_r, win_c):
    kr, row_start, win_start, n_win = _na_window(rows, win_r)
    n_blk = rows // NA_ROW_BLOCK
    cols = np.arange(GRID_W)
    col_start = np.clip(cols - win_c // 2, 0, GRID_W - win_c)
    qr = np.arange(rows).reshape(n_blk, NA_ROW_BLOCK)
    kr_abs = win_start[:, None] + np.arange(n_win)[None, :]
    row_ok = ((kr_abs[:, None, :] >= row_start[qr][:, :, None])
              & (kr_abs[:, None, :] < row_start[qr][:, :, None] + kr))
    row_idx = np.clip(kr_abs[:, None, :] - qr[:, :, None] + win_r - 1, 0, 2 * win_r - 2)
    col_ok = (cols[None, :] >= col_start[:, None]) & (cols[None, :] < col_start[:, None] + win_c)
    col_idx = np.clip(cols[None, :] - cols[:, None] + win_c - 1, 0, 2 * win_c - 2)
    shape = (n_blk, NA_ROW_BLOCK, GRID_W, n_win, GRID_W)
    ok = np.broadcast_to(row_ok[:, :, None, :, None] & col_ok[None, None, :, None, :], shape)
    ridx = np.broadcast_to(row_idx[:, :, None, :, None], shape)
    cidx = np.broadcast_to(col_idx[None, None, :, None, :], shape)
    flat = lambda a: a.reshape(n_blk, NA_ROW_BLOCK * GRID_W, n_win * GRID_W)
    ok, ridx, cidx = flat(ok), flat(ridx), flat(cidx)
    types, type_of = [], np.zeros(n_blk, np.int32)
    for j in range(n_blk):
        for t, jt in enumerate(types):
            if (np.array_equal(ok[j], ok[jt]) and np.array_equal(ridx[j], ridx[jt])):
                type_of[j] = t
                break
        else:
            type_of[j] = len(types)
            types.append(j)
    sel = np.asarray(types)
    table = jnp.where(jnp.asarray(ok[sel])[None], rpb[:, ridx[sel], cidx[sel]].astype(F32), NEG_MASK)
    return table, type_of, win_start, n_win


def _na_kernel(type_ref, start_ref, q_ref, k_ref, v_ref, ck_ref, cv_ref, bias_ref, o_ref, *, n_keys, scale):
    del type_ref
    j = pl.program_id(2)
    start = pl.multiple_of(start_ref[j] * GRID_W, GRID_W)
    nt = (((1,), (1,)), ((), ()))
    q = q_ref[...].astype(BF16)
    kw = k_ref[pl.ds(start, n_keys), :].astype(BF16)
    vw = v_ref[pl.ds(start, n_keys), :].astype(BF16)
    ck = ck_ref[...].astype(BF16)
    cv = cv_ref[...].astype(BF16)
    s_loc = lax.dot_general(q, kw, nt, preferred_element_type=F32) * scale + bias_ref[...]
    s_ctx = lax.dot_general(q, ck, nt, preferred_element_type=F32) * scale
    m = jnp.maximum(jnp.max(s_loc, axis=-1, keepdims=True), jnp.max(s_ctx, axis=-1, keepdims=True))
    p_loc = jnp.exp(s_loc - m)
    p_ctx = jnp.exp(s_ctx - m)
    l = jnp.sum(p_loc, axis=-1, keepdims=True) + jnp.sum(p_ctx, axis=-1, keepdims=True)
    o = (jnp.dot(p_loc.astype(BF16), vw, preferred_element_type=F32)
         + jnp.dot(p_ctx.astype(BF16), cv, preferred_element_type=F32)) / l
    o_ref[...] = o.astype(o_ref.dtype)


def _neighbourhood_attention(p_arr, cache_k, cache_v, rpb, batch, seq, n_heads, hd):
    rows = seq // GRID_W
    win_r = (rpb.shape[1] + 1) // 2
    win_c = (rpb.shape[2] + 1) // 2
    table, type_of, win_start, n_win = _na_bias_tables(rpb, rows, win_r, win_c)
    n_blk = rows // NA_ROW_BLOCK
    tq = NA_ROW_BLOCK * GRID_W
    n_keys = n_win * GRID_W
    past = cache_k.shape[0] // batch
    grid_spec = pltpu.PrefetchScalarGridSpec(
        num_scalar_prefetch=2,
        grid=(batch, n_heads, n_blk),
        in_specs=[
            pl.BlockSpec((tq, hd), lambda b, h, j, t, s: (b * n_blk + j, h)),
            pl.BlockSpec((seq, hd), lambda b, h, j, t, s: (b, n_heads + h)),
            pl.BlockSpec((seq, hd), lambda b, h, j, t, s: (b, 2 * n_heads + h)),
            pl.BlockSpec((past, hd), lambda b, h, j, t, s: (b, h)),
            pl.BlockSpec((past, hd), lambda b, h, j, t, s: (b, h)),
            pl.BlockSpec((None, None, tq, n_keys), lambda b, h, j, t, s: (h, t[j], 0, 0)),
        ],
        out_specs=pl.BlockSpec((tq, hd), lambda b, h, j, t, s: (b * n_blk + j, h)),
    )
    return pl.pallas_call(
        functools.partial(_na_kernel, n_keys=n_keys, scale=hd ** -0.5),
        grid_spec=grid_spec,
        out_shape=jax.ShapeDtypeStruct((batch * seq, n_heads * hd), BF16),
        compiler_params=_params(("parallel", "parallel", "arbitrary"),
                                4 * _nbytes((seq, hd), F32), 4 * _nbytes((past, hd), F32),
                                8 * _nbytes((tq, n_keys), F32), 4 * _nbytes((tq, past), F32),
                                8 * _nbytes((tq, hd), F32)),
        name="neighbourhood_attention",
    )(jnp.asarray(type_of), jnp.asarray(win_start.astype(np.int32)), p_arr, p_arr, p_arr,
      cache_k, cache_v, table)


def _softplus(x):
    return jnp.maximum(x, 0.0) + jnp.log1p(jnp.exp(-jnp.abs(x)))


def _lru_kernel(xb_ref, gb_ref, h0_ref, cw_ref, cb_ref, wa_ref, wx_ref, ba_ref, bx_ref, lam_ref,
                y_ref, st_ref, xpad, a_f, u_f, a_b, u_b, y_f, y_b, *, seq, tc):
    c = xb_ref.shape[1]
    n_chunk = seq // tc
    n_tile = seq // SUBLANES
    pad = SUBLANES

    xpad[0:pad, :] = jnp.zeros((pad, c), F32)
    xpad[seq + pad:seq + 2 * pad, :] = jnp.zeros((pad, c), F32)

    def copy_body(i, carry):
        s = pl.multiple_of(i * tc, tc)
        xpad[pl.ds(s + pad, tc), :] = xb_ref[pl.ds(s, tc), :]
        return carry

    lax.fori_loop(0, n_chunk, copy_body, 0)

    cw = cw_ref[...]
    cb = cb_ref[...]
    soft = [_softplus(-lam_ref[d:d + 1, :]) for d in range(2)]
    in_tile = lax.broadcasted_iota(jnp.int32, (tc, c), 0) % SUBLANES
    win_rows = tc + 2 * pad

    def gate_body(i, carry):
        s = pl.multiple_of(i * tc, tc)
        win = xpad[pl.ds(s, win_rows), :]
        mid = lambda a: a[pad:pad + tc]
        xc = mid(pltpu.roll(win, 2, 0)) * cw[0:1]
        xc = xc + mid(pltpu.roll(win, 1, 0)) * cw[1:2]
        xc = xc + mid(win) * cw[2:3]
        xc = xc + mid(pltpu.roll(win, win_rows - 1, 0)) * cw[3:4]
        xc = xc + cb
        xcb = xc.astype(BF16)
        for d, (a_scr, u_scr) in enumerate(((a_f, u_f), (a_b, u_b))):
            r = jax.nn.sigmoid(jnp.dot(xcb, wa_ref[d], preferred_element_type=F32) + ba_ref[d:d + 1, :])
            g = jax.nn.sigmoid(jnp.dot(xcb, wx_ref[d], preferred_element_type=F32) + bx_ref[d:d + 1, :])
            log_a = (-LRU_C * r) * soft[d]
            a = jnp.exp(log_a)
            u = jnp.sqrt(-jnp.tanh(log_a) * (a * a + 1.0)) * (g * xc)
            for sh in (1, 2, 4):
                if d == 0:
                    keep = in_tile >= sh
                    a_prev = pltpu.roll(a, sh, 0)
                    u_prev = pltpu.roll(u, sh, 0)
                else:
                    keep = in_tile < SUBLANES - sh
                    a_prev = pltpu.roll(a, tc - sh, 0)
                    u_prev = pltpu.roll(u, tc - sh, 0)
                u = a * jnp.where(keep, u_prev, 0.0) + u
                a = a * jnp.where(keep, a_prev, 1.0)
            a_scr[pl.ds(s, tc), :] = a
            u_scr[pl.ds(s, tc), :] = u
        return carry

    lax.fori_loop(0, n_chunk, gate_body, 0)

    def scan_body(j, carry):
        h_f, h_b = carry
        sf = pl.multiple_of(j * SUBLANES, SUBLANES)
        sb = pl.multiple_of((n_tile - 1 - j) * SUBLANES, SUBLANES)
        f = u_f[pl.ds(sf, SUBLANES), :] + a_f[pl.ds(sf, SUBLANES), :] * h_f
        b = u_b[pl.ds(sb, SUBLANES), :] + a_b[pl.ds(sb, SUBLANES), :] * h_b
        y_f[pl.ds(sf, SUBLANES), :] = f
        y_b[pl.ds(sb, SUBLANES), :] = b
        return (jnp.broadcast_to(f[SUBLANES - 1:SUBLANES, :], (SUBLANES, c)),
                jnp.broadcast_to(b[0:1, :], (SUBLANES, c)))

    h0 = h0_ref[...]
    h_f, h_b = lax.fori_loop(
        0, n_tile, scan_body,
        (jnp.broadcast_to(h0[0:1, :], (SUBLANES, c)), jnp.broadcast_to(h0[1:2, :], (SUBLANES, c))),
        unroll=8)
    st_ref[0:1, :] = h_f[0:1, :]
    st_ref[1:2, :] = h_b[0:1, :]

    def out_body(i, carry):
        s = pl.multiple_of(i * tc, tc)
        y = y_f[pl.ds(s, tc), :] + y_b[pl.ds(s, tc), :]
        y_ref[pl.ds(s, tc), :] = (y * jax.nn.gelu(gb_ref[pl.ds(s, tc), :])).astype(y_ref.dtype)
        return carry

    lax.fori_loop(0, n_chunk, out_body, 0)


def _rglru(p_arr, xb_col0, gb_col0, h0, conv_w, conv_b, wa, ba, wx, bx, lam, batch, seq):
    _, n_blk, c, _ = wa.shape
    w_b = n_blk * c
    tc = _pick(seq, 512, SUBLANES)
    col = lambda b, n: (0, n)
    full = lambda b, n: (0, n, 0, 0)
    scratch = [pltpu.VMEM((seq + 2 * SUBLANES, c), F32)] + [pltpu.VMEM((seq, c), F32)] * 6
    return pl.pallas_call(
        functools.partial(_lru_kernel, seq=seq, tc=tc),
        grid=(batch, n_blk),
        in_specs=[
            pl.BlockSpec((seq, c), lambda b, n: (b, xb_col0 + n)),
            pl.BlockSpec((seq, c), lambda b, n: (b, gb_col0 + n)),
            pl.BlockSpec((None, 2, c), lambda b, n: (b, 0, n)),
            pl.BlockSpec((conv_w.shape[0], c), col),
            pl.BlockSpec((1, c), col),
            pl.BlockSpec((2, None, c, c), full),
            pl.BlockSpec((2, None, c, c), full),
            pl.BlockSpec((2, c), col),
            pl.BlockSpec((2, c), col),
            pl.BlockSpec((2, c), col),
        ],
        out_specs=[
            pl.BlockSpec((seq, c), lambda b, n: (b, n)),
            pl.BlockSpec((None, 2, c), lambda b, n: (b, 0, n)),
        ],
        out_shape=[jax.ShapeDtypeStruct((batch * seq, w_b), BF16),
                   jax.ShapeDtypeStruct((batch, 2, w_b), F32)],
        scratch_shapes=scratch,
        compiler_params=_params(("parallel", "parallel"),
                                4 * _nbytes((seq, c), F32), 2 * _nbytes((seq, c), BF16),
                                7 * _nbytes((seq + 2 * SUBLANES, c), F32), 24 * _nbytes((tc, c), F32)),
        name="rglru",
    )(p_arr, p_arr, h0, conv_w, conv_b.reshape(1, w_b), wa.astype(BF16), wx.astype(BF16), ba, bx, lam)


def _rope_tables(seq, hd):
    half = hd // 2
    quarter = half // 2
    t = np.arange(seq)
    inv = ROPE_THETA ** (-np.arange(0, half, 2, dtype=np.float32) / half)
    ang_r = (t // GRID_W).astype(np.float32)[:, None] * inv
    ang_c = (t % GRID_W).astype(np.float32)[:, None] * inv
    cos = jnp.concatenate([jnp.cos(ang_r)] * 2 + [jnp.cos(ang_c)] * 2, axis=-1)
    sin = jnp.concatenate([jnp.sin(ang_r)] * 2 + [jnp.sin(ang_c)] * 2, axis=-1)
    first = (np.arange(hd) % half) < quarter
    s_minus = jnp.where(first, -sin, 0.0)
    s_plus = jnp.where(first, 0.0, sin)
    return cos.astype(F32), s_minus.astype(F32), s_plus.astype(F32)


def _qk_prep_kernel(*refs, n_q, n_kv, hd, rope):
    if rope:
        q_ref, k_ref, v_ref, qg_ref, kg_ref, c_ref, sm_ref, sp_ref, qo_ref, ko_ref, vo_ref = refs
        cos, s_minus, s_plus = c_ref[...], sm_ref[...], sp_ref[...]
    else:
        q_ref, k_ref, v_ref, qg_ref, kg_ref, qo_ref, ko_ref, vo_ref = refs
    quarter = hd // 4

    def prep(x, g):
        y = x * lax.rsqrt(jnp.mean(x * x, axis=-1, keepdims=True) + EPS) * g
        if rope:
            y = y * cos + pltpu.roll(y, hd - quarter, 1) * s_minus + pltpu.roll(y, quarter, 1) * s_plus
        return y

    qg = qg_ref[...]
    kg = kg_ref[...]
    for h in range(n_q):
        sl = slice(h * hd, (h + 1) * hd)
        qo_ref[:, sl] = prep(q_ref[:, sl], qg).astype(qo_ref.dtype)
    for h in range(n_kv):
        sl = slice(h * hd, (h + 1) * hd)
        ko_ref[:, sl] = prep(k_ref[:, sl], kg).astype(ko_ref.dtype)
    vo_ref[...] = v_ref[...].astype(vo_ref.dtype)


def _qk_prep(p_arr, q_g, k_g, n_q, n_kv, hd, seq, rope, k_dtype):
    m = p_arr.shape[0]
    tt = _pick(seq, 256, SUBLANES)
    wq, wk = n_q * hd, n_kv * hd
    assert wq % wk == 0
    in_specs = [
        pl.BlockSpec((tt, wq), lambda i: (i, 0)),
        pl.BlockSpec((tt, wk), lambda i: (i, wq // wk)),
        pl.BlockSpec((tt, wk), lambda i: (i, wq // wk + 1)),
        pl.BlockSpec((1, hd), lambda i: (0, 0)),
        pl.BlockSpec((1, hd), lambda i: (0, 0)),
    ]
    args = [p_arr, p_arr, p_arr, q_g.reshape(1, hd), k_g.reshape(1, hd)]
    if rope:
        tiles = seq // tt
        in_specs += [pl.BlockSpec((tt, hd), lambda i: (i % tiles, 0))] * 3
        args += list(_rope_tables(seq, hd))
    return pl.pallas_call(
        functools.partial(_qk_prep_kernel, n_q=n_q, n_kv=n_kv, hd=hd, rope=rope),
        grid=(m // tt,),
        in_specs=in_specs,
        out_specs=[pl.BlockSpec((tt, wq), lambda i: (i, 0)),
                   pl.BlockSpec((tt, wk), lambda i: (i, 0)),
                   pl.BlockSpec((tt, wk), lambda i: (i, 0))],
        out_shape=[jax.ShapeDtypeStruct((m, wq), BF16),
                   jax.ShapeDtypeStruct((m, wk), k_dtype),
                   jax.ShapeDtypeStruct((m, wk), BF16)],
        compiler_params=_params(("parallel",), 4 * _nbytes((tt, wq + 2 * wk), F32),
                                4 * _nbytes((tt, wq + 2 * wk), F32)),
        name="qk_norm_rope",
    )(*args)


LOG2E = 1.4426950408889634


def _flash_kernel(q_ref, k_ref, v_ref, ck_ref, cv_ref, o_ref, m_scr, l_scr, acc_scr, *,
                  group, hd, tq, tk, n_chunks, scale):
    nt = (((1,), (1,)), ((), ()))
    q = jnp.concatenate([q_ref[:, g * hd:(g + 1) * hd] for g in range(group)], axis=0)
    m_scr[...] = jnp.full(m_scr.shape, -jnp.inf, F32)
    l_scr[...] = jnp.zeros(l_scr.shape, F32)
    acc_scr[...] = jnp.zeros(acc_scr.shape, F32)
    c2 = scale * LOG2E

    def update(kc, vc):
        s = lax.dot_general(q, kc, nt, preferred_element_type=F32) * c2
        m_prev = m_scr[...]
        m_new = jnp.maximum(m_prev, jnp.max(s, axis=-1, keepdims=True))
        alpha = jnp.exp2(m_prev - m_new)
        p = jnp.exp2(s - m_new)
        l_scr[...] = alpha * l_scr[...] + jnp.sum(p, axis=-1, keepdims=True)
        acc_scr[...] = alpha * acc_scr[...] + jnp.dot(p.astype(BF16), vc, preferred_element_type=F32)
        m_scr[...] = m_new

    def body(i, carry):
        s = pl.multiple_of(i * tk, tk)
        update(k_ref[pl.ds(s, tk), :], v_ref[pl.ds(s, tk), :])
        return carry

    lax.fori_loop(0, n_chunks, body, 0)
    update(ck_ref[...].astype(BF16), cv_ref[...].astype(BF16))
    o = acc_scr[...] / l_scr[...]
    for g in range(group):
        o_ref[:, g * hd:(g + 1) * hd] = o[g * tq:(g + 1) * tq].astype(o_ref.dtype)


def _flash_gqa(q, k, v, cache_k, cache_v, batch, seq, n_q, n_kv, hd):
    group = n_q // n_kv
    past = cache_k.shape[0] // batch
    tq = _pick(seq, 256, SUBLANES)
    tk = _pick(seq, 512, LANES)
    n_qt = seq // tq
    rows = group * tq
    return pl.pallas_call(
        functools.partial(_flash_kernel, group=group, hd=hd, tq=tq, tk=tk, n_chunks=seq // tk,
                          scale=hd ** -0.5),
        grid=(batch, n_kv, n_qt),
        in_specs=[
            pl.BlockSpec((tq, group * hd), lambda b, h, i: (b * n_qt + i, h)),
            pl.BlockSpec((seq, hd), lambda b, h, i: (b, h)),
            pl.BlockSpec((seq, hd), lambda b, h, i: (b, h)),
            pl.BlockSpec((past, hd), lambda b, h, i: (b, h)),
            pl.BlockSpec((past, hd), lambda b, h, i: (b, h)),
        ],
        out_specs=pl.BlockSpec((tq, group * hd), lambda b, h, i: (b * n_qt + i, h)),
        out_shape=jax.ShapeDtypeStruct((batch * seq, n_q * hd), BF16),
        scratch_shapes=[pltpu.VMEM((rows, 1), F32), pltpu.VMEM((rows, 1), F32), pltpu.VMEM((rows, hd), F32)],
        compiler_params=_params(("parallel", "parallel", "arbitrary"),
                                4 * _nbytes((seq, hd), BF16), 4 * _nbytes((past, hd), F32),
                                8 * _nbytes((rows, hd), F32), 6 * _nbytes((rows, tk), F32)),
        name="flash_gqa",
    )(q, k, v, cache_k, cache_v)


def _pad_last(a, n):
    return a if a.shape[-1] == n else jnp.pad(a, [(0, 0)] * (a.ndim - 1) + [(0, n - a.shape[-1])])


def _pad_axis(a, axis, n):
    if a.shape[axis] == n:
        return a
    pads = [(0, 0)] * a.ndim
    pads[axis] = (0, n - a.shape[axis])
    return jnp.pad(a, pads)


def kernel(x_prompt, x_sample, cache_a_k, cache_a_v, state_lru, cache_c_k, cache_c_v, c, c_ctx,
           ada_w, ada_b, norm_g, ffn_w1, ffn_w3, ffn_w2, ev_w_in, ev_w_out, na_rpb,
           lru_conv_w, lru_conv_b, lru_wa, lru_ba, lru_wx, lru_bx, lru_lambda,
           od_w_in, od_w_out, c_q_g, c_k_g, norm_f):
    batch, seq, d = x_prompt.shape
    dec_batch, dec_seq, _ = x_sample.shape
    depth = ada_w.shape[0]
    hd = cache_a_k.shape[-1]
    h_a = cache_a_k.shape[3]
    w_a = h_a * hd
    lru_blk = lru_wa.shape[-1]
    kv_c = cache_c_k.shape[3]
    h_c = od_w_out.shape[1] // hd
    d_ff = ffn_w1.shape[-1]
    assert 1 + dec_batch <= COND_ROWS and dec_seq % GRID_W == 0 and (dec_seq // GRID_W) % NA_ROW_BLOCK == 0

    cond = jnp.concatenate([c_ctx[None], c, jnp.zeros((COND_ROWS - 1 - dec_batch, d), F32)], axis=0)
    mod = _modulation(cond, ada_w, ada_b)
    mod3 = mod.reshape(depth * COND_ROWS * N_MOD, 1, d)

    f_pad = -(-d_ff // 1024) * 1024
    w1 = _pad_last(ffn_w1.astype(BF16), f_pad).reshape(depth * 2, d, f_pad)
    w3 = _pad_last(ffn_w3.astype(BF16), f_pad).reshape(depth * 2, d, f_pad)
    w2 = _pad_axis(ffn_w2.astype(BF16), 2, f_pad).reshape(depth * 2, f_pad, d)
    ev_in = ev_w_in.astype(BF16)
    ev_out = ev_w_out.astype(BF16)
    od_in = od_w_in.astype(BF16)
    od_out = od_w_out.astype(BF16)

    streams = (_Stream(0, batch * seq), _Stream(1, dec_seq))
    xs = [x_prompt.reshape(batch * seq, d), x_sample.reshape(dec_batch * dec_seq, d)]

    def half_ffn(x, stream, layer, slot, li):
        h = _adaln(x, norm_g[layer, slot], mod3, stream, layer, slot)
        g = _gateup(h, w1, w3, li)
        return _matmul_residual(g, w2, li, x, mod3, stream, layer, slot, 0.5, f_pad // 4, 1024)

    new_a_k, new_a_v, new_st, new_c_k, new_c_v = [], [], [], [], []
    for layer in range(depth):
        xs = [half_ffn(x, s, layer, 0, 2 * layer) for x, s in zip(xs, streams)]
        hs = [_adaln(x, norm_g[layer, 1], mod3, s, layer, 1) for x, s in zip(xs, streams)]
        if layer % 2 == 0:
            e = layer // 2
            pp = _proj(hs[0], ev_in[e])
            ps = _proj(hs[1], ev_in[e])
            lru = (lru_conv_w[e], lru_conv_b[e], lru_wa[e], lru_ba[e], lru_wx[e], lru_bx[e], lru_lambda[e])
            xb0, gb0 = 3 * w_a // lru_blk, (3 * w_a + lru_wa.shape[2] * lru_blk) // lru_blk
            oa_p = _ctx_attention(pp, 0, pp, h_a, pp, 2 * h_a, batch, seq, h_a, 1, hd)
            yg_p, st = _rglru(pp, xb0, gb0, jnp.zeros((batch, 2, lru_wa.shape[2] * lru_blk), F32), *lru,
                              batch, seq)
            new_a_k.append(pp[:, w_a:2 * w_a].reshape(batch, seq, h_a, hd))
            new_a_v.append(pp[:, 2 * w_a:3 * w_a].reshape(batch, seq, h_a, hd))
            new_st.append(st)
            ck = cache_a_k[:, e].reshape(dec_batch * cache_a_k.shape[2], w_a)
            cv = cache_a_v[:, e].reshape(dec_batch * cache_a_v.shape[2], w_a)
            oa_s = _neighbourhood_attention(ps, ck, cv, na_rpb[e], dec_batch, dec_seq, h_a, hd)
            yg_s, _ = _rglru(ps, xb0, gb0, state_lru[:, e].astype(F32), *lru, dec_batch, dec_seq)
            merged = [jnp.concatenate([oa_p, yg_p], axis=-1), jnp.concatenate([oa_s, yg_s], axis=-1)]
            w_out, wi = ev_out, e
        else:
            o = layer // 2
            pp = _proj(hs[0], od_in[o])
            ps = _proj(hs[1], od_in[o])
            q_p, k_p, v_p = _qk_prep(pp, c_q_g[o], c_k_g[o], h_c, kv_c, hd, seq, False, F32)
            o_p = _ctx_attention(q_p, 0, k_p, 0, v_p, 0, batch, seq, h_c, h_c // kv_c, hd)
            new_c_k.append(k_p.reshape(batch, seq, kv_c, hd))
            new_c_v.append(pp[:, (h_c + kv_c) * hd:].reshape(batch, seq, kv_c, hd))
            q_s, k_s, v_s = _qk_prep(ps, c_q_g[o], c_k_g[o], h_c, kv_c, hd, dec_seq, True, BF16)
            ck = cache_c_k[:, o].reshape(dec_batch * cache_c_k.shape[2], kv_c * hd)
            cv = cache_c_v[:, o].reshape(dec_batch * cache_c_v.shape[2], kv_c * hd)
            o_s = _flash_gqa(q_s, k_s, v_s, ck, cv, dec_batch, dec_seq, h_c, kv_c, hd)
            merged = [o_p, o_s]
            w_out, wi = od_out, o
        xs = [_matmul_residual(mg, w_out, wi, x, mod3, s, layer, 1, 1.0, mg.shape[1], 512)
              for mg, x, s in zip(merged, xs, streams)]
        xs = [half_ffn(x, s, layer, 2, 2 * layer + 1) for x, s in zip(xs, streams)]

    y_prompt = _final_norm(xs[0], norm_f).reshape(batch, seq, d)
    y_sample = _final_norm(xs[1], norm_f).reshape(dec_batch, dec_seq, d)
    return (y_prompt, y_sample,
            jnp.stack(new_a_k, axis=1), jnp.stack(new_a_v, axis=1), jnp.stack(new_st, axis=1),
            jnp.stack(new_c_k, axis=1), jnp.stack(new_c_v, axis=1))
```
